```python
import jax
import jax.numpy as jnp
from jax import lax
import numpy as np


D_MODEL = 1024
BATCH = 4
SEQ = 4096
DEPTH = 2

HEAD_DIM = 64
A_WIDTH = D_MODEL // 2
A_HEADS = A_WIDTH // HEAD_DIM
CHUNK = 128
B_WIDTH = D_MODEL // 2
CONV_WIDTH = 31
IN_WIDTH = 2 * A_WIDTH + 2 * B_WIDTH
ATTN_HEADS = D_MODEL // HEAD_DIM
DILATED_PATTERNS = ((128, 1), (512, 4), (2048, 16))
BLOCK = 128
ROPE_THETA = 10000.0
D_FF = 2816
FFN_CONV_WIDTH = 3
EPS = 1e-6
NEG = -1e30

kernel_name = 'hybrid_gmlp_conformer_dilated_attn_block'


def rms_norm(x, g):
    xf = x.astype(jnp.float32)
    y = xf * lax.rsqrt(jnp.mean(xf * xf, axis=-1, keepdims=True) + EPS)
    return (y * g.astype(jnp.float32)).astype(x.dtype)


def layer_norm(x, g, b):
    xf = x.astype(jnp.float32)
    mu = jnp.mean(xf, axis=-1, keepdims=True)
    var = jnp.mean(jnp.square(xf - mu), axis=-1, keepdims=True)
    y = (xf - mu) * lax.rsqrt(var + EPS)
    return (y * g.astype(jnp.float32) + b.astype(jnp.float32)).astype(x.dtype)


def causal_dwconv(x, w, b):
    k = w.shape[0]
    y = lax.conv_general_dilated(
        x, w[:, None, :].astype(x.dtype), window_strides=(1,),
        padding=[(k - 1, 0)], dimension_numbers=('NWC', 'WIO', 'NWC'),
        feature_group_count=x.shape[-1])
    return y + b.astype(x.dtype)


def rotary(x, pos):
    half = x.shape[-1] // 2
    inv = ROPE_THETA ** (-jnp.arange(half, dtype=jnp.float32) / half)
    ang = pos.astype(jnp.float32)[:, None] * inv[None, :]
    cos = jnp.cos(ang)[None, :, None, :]
    sin = jnp.sin(ang)[None, :, None, :]
    xf = x.astype(jnp.float32)
    x1, x2 = xf[..., :half], xf[..., half:]
    out = jnp.concatenate([x1 * cos - x2 * sin, x2 * cos + x1 * sin], axis=-1)
    return out.astype(x.dtype)


def chunked_spatial_gating(z, ln_g, ln_b, w_s, b_s):
    u, v = jnp.split(z, 2, axis=-1)
    v = layer_norm(v, ln_g, ln_b)
    bn, s, _ = v.shape
    v = v.reshape(bn, s // CHUNK, CHUNK, A_HEADS, HEAD_DIM)
    causal = jnp.tril(jnp.ones((CHUNK, CHUNK), dtype=bool))
    w = jnp.where(causal, w_s, jnp.zeros_like(w_s))
    mixed = jnp.einsum('hts,bcshd->bcthd', w, v) + b_s.T[None, None, :, :, None]
    return u * mixed.reshape(bn, s, A_WIDTH)


def conformer_conv(z, conv_w, conv_b, ln_g, ln_b):
    a, g = jnp.split(z, 2, axis=-1)
    h = a * jax.nn.sigmoid(g)
    h = causal_dwconv(h, conv_w, conv_b)
    h = layer_norm(h, ln_g, ln_b)
    return jax.nn.silu(h)


def dilated_branch(q, k, v, window, dilation):
    bn, s, h, dh = q.shape
    span = dilation * BLOCK
    s_pad = -(-s // span) * span
    seq_len = s_pad // dilation
    nb = seq_len // BLOCK
    reach = window // dilation

    def strided(t):
        t = jnp.pad(t, [(0, 0), (0, s_pad - s), (0, 0), (0, 0)])
        t = t.reshape(bn, seq_len, dilation, h, dh).transpose(0, 2, 1, 3, 4)
        return t.reshape(bn, dilation, nb, BLOCK, h, dh)

    def with_prev(t):
        prev = jnp.pad(t, [(0, 0), (0, 0), (1, 0), (0, 0), (0, 0), (0, 0)])[:, :, :-1]
        return jnp.concatenate([prev, t], axis=3)

    qs = strided(q)
    kw = with_prev(strided(k))
    vw = with_prev(strided(v))
    qi = jnp.arange(BLOCK)[:, None]
    kj = jnp.arange(2 * BLOCK)[None, :]
    dist = BLOCK + qi - kj
    band = (dist >= 0) & (dist <= reach)
    blk = jnp.arange(nb)[:, None, None]
    valid = band[None] & ((blk > 0) | (kj[None] >= BLOCK))

    scores = jnp.einsum('brnqhd,brnkhd->brnhqk', qs, kw).astype(jnp.float32)
    scores = scores * (HEAD_DIM ** -0.5)
    scores = jnp.where(valid[None, None, :, None], scores, NEG)
    m = jnp.max(scores, axis=-1, keepdims=True)
    p = jnp.exp(scores - m)
    l = jnp.sum(p, axis=-1, keepdims=True)
    o = jnp.einsum('brnhqk,brnkhd->brnqhd', p / l, vw.astype(jnp.float32))
    lse = (m + jnp.log(l))[..., 0]

    o = o.reshape(bn, dilation, seq_len, h, dh).transpose(0, 2, 1, 3, 4)
    o = o.reshape(bn, s_pad, h, dh)[:, :s]
    lse = lse.transpose(0, 1, 2, 4, 3).reshape(bn, dilation, seq_len, h)
    lse = lse.transpose(0, 2, 1, 3).reshape(bn, s_pad, h)[:, :s]
    return o, lse


def dilated_attention(q, k, v):
    outs, lses = [], []
    for window, dilation in DILATED_PATTERNS:
        o, lse = dilated_branch(q, k, v, window, dilation)
        outs.append(o)
        lses.append(lse)
    wts = jax.nn.softmax(jnp.stack(lses, axis=0), axis=0)
    return jnp.einsum('pbsh,pbshd->bshd', wts, jnp.stack(outs, axis=0))


def conv_ffn(x, norm_g, w_up, conv_w, conv_b, w_down):
    h = rms_norm(x, norm_g)
    u = causal_dwconv(h @ w_up, conv_w, conv_b)
    gate, val = jnp.split(u, 2, axis=-1)
    return x + (jax.nn.silu(gate) * val) @ w_down


def setup_inputs(seed: int = 0) -> dict:
    key = jax.random.key(seed)
    keys = iter(jax.random.split(key, 32))
    n_even = (DEPTH + 1) // 2
    n_odd = DEPTH // 2
    f32 = jnp.float32

    def nrm(shape, scale):
        return jax.random.normal(next(keys), shape, f32) * scale

    def gain(shape):
        return 1.0 + nrm(shape, 0.02)

    return {
        'x': nrm((BATCH, SEQ, D_MODEL), 1.0),
        'even_norm_g': gain((n_even, D_MODEL)),
        'even_w_in': nrm((n_even, D_MODEL, IN_WIDTH), D_MODEL ** -0.5),
        'even_b_in': nrm((n_even, IN_WIDTH), 0.02),
        'even_v_ln_g': gain((n_even, A_WIDTH)),
        'even_v_ln_b': nrm((n_even, A_WIDTH), 0.02),
        'even_w_s': nrm((n_even, A_HEADS, CHUNK, CHUNK), CHUNK ** -0.5),
        'even_b_s': gain((n_even, A_HEADS, CHUNK)),
        'even_conv_w': nrm((n_even, CONV_WIDTH, B_WIDTH), CONV_WIDTH ** -0.5),
        'even_conv_b': nrm((n_even, B_WIDTH), 0.02),
        'even_conv_ln_g': gain((n_even, B_WIDTH)),
        'even_conv_ln_b': nrm((n_even, B_WIDTH), 0.02),
        'even_w_out': nrm((n_even, A_WIDTH + B_WIDTH, D_MODEL), (A_WIDTH + B_WIDTH) ** -0.5),
        'odd_norm_g': gain((n_odd, D_MODEL)),
        'odd_w_qkv': nrm((n_odd, D_MODEL, 3 * D_MODEL), D_MODEL ** -0.5),
        'odd_w_o': nrm((n_odd, D_MODEL, D_MODEL), D_MODEL ** -0.5),
        'ffn_norm_g': gain((DEPTH, D_MODEL)),
        'ffn_w_up': nrm((DEPTH, D_MODEL, 2 * D_FF), D_MODEL ** -0.5),
        'ffn_conv_w': nrm((DEPTH, FFN_CONV_WIDTH, 2 * D_FF), FFN_CONV_WIDTH ** -0.5),
        'ffn_conv_b': nrm((DEPTH, 2 * D_FF), 0.02),
        'ffn_w_down': nrm((DEPTH, D_FF, D_MODEL), D_FF ** -0.5),
        'final_norm_g': gain((D_MODEL,)),
    }


def reference(x, even_norm_g, even_w_in, even_b_in, even_v_ln_g, even_v_ln_b,
              even_w_s, even_b_s, even_conv_w, even_conv_b, even_conv_ln_g,
              even_conv_ln_b, even_w_out, odd_norm_g, odd_w_qkv, odd_w_o,
              ffn_norm_g, ffn_w_up, ffn_conv_w, ffn_conv_b, ffn_w_down,
              final_norm_g):
    bn, s, _ = x.shape
    pos = jnp.arange(s)
    for i in range(DEPTH):
        j = i // 2
        if i % 2 == 0:
            h = rms_norm(x, even_norm_g[j])
            z = h @ even_w_in[j] + even_b_in[j]
            za = jax.nn.gelu(z[..., :2 * A_WIDTH])
            zb = z[..., 2 * A_WIDTH:]
            ya = chunked_spatial_gating(za, even_v_ln_g[j], even_v_ln_b[j],
                                        even_w_s[j], even_b_s[j])
            yb = conformer_conv(zb, even_conv_w[j], even_conv_b[j],
                                even_conv_ln_g[j], even_conv_ln_b[j])
            x = x + jnp.concatenate([ya, yb], axis=-1) @ even_w_out[j]
        else:
            h = rms_norm(x, odd_norm_g[j])
            qkv = (h @ odd_w_qkv[j]).reshape(bn, s, 3, ATTN_HEADS, HEAD_DIM)
            q = rotary(qkv[:, :, 0], pos)
            k = rotary(qkv[:, :, 1], pos)
            v = qkv[:, :, 2]
            o = dilated_attention(q, k, v).astype(x.dtype).reshape(bn, s, D_MODEL)
            x = x + o @ odd_w_o[j]
        x = conv_ffn(x, ffn_norm_g[i], ffn_w_up[i], ffn_conv_w[i], ffn_conv_b[i],
                     ffn_w_down[i])
    return rms_norm(x, final_norm_g)
```

```python
import functools

import jax
import jax.numpy as jnp
from jax import lax
from jax.experimental import pallas as pl
from jax.experimental.pallas import tpu as pltpu

F32 = jnp.float32
BF16 = jnp.bfloat16

D_MODEL = 1024
HEAD_DIM = 64
A_WIDTH = 512
CHUNK = 128
B_WIDTH = 512
CONV_WIDTH = 31
IN_WIDTH = 2 * A_WIDTH + 2 * B_WIDTH
DILATED_PATTERNS = ((128, 1), (512, 4), (2048, 16))
BLOCK = 128
ROPE_THETA = 10000.0
D_FF = 2816
FFN_CONV_WIDTH = 3
EPS = 1e-6
NEG = -1e30

LANES = 128
SUBLANES = 8
MXU_DIM = 256
ROW_TILE = 512
MM_ROWS = 256
HALO = 32
ATTN_TILE = 2048
FF_CHUNK = 256
VMEM_LIMIT = 56 * 1024 * 1024


def _dot(a, b):
    return jnp.dot(a, b, preferred_element_type=F32)


def _dot_nt(a, b):
    return lax.dot_general(a, b, (((1,), (1,)), ((), ())), preferred_element_type=F32)


def _rms(x, g):
    ms = jnp.mean(x * x, axis=-1, keepdims=True)
    return x * lax.rsqrt(ms + EPS) * g


def _layer_norm(x, g, b):
    mu = jnp.mean(x, axis=-1, keepdims=True)
    xc = x - mu
    var = jnp.mean(xc * xc, axis=-1, keepdims=True)
    return xc * lax.rsqrt(var + EPS) * g + b


def _const_spec(shape):
    return pl.BlockSpec(shape, lambda *_: (0,) * len(shape), pipeline_mode=pl.Buffered(1))


def _params(sem):
    return pltpu.CompilerParams(dimension_semantics=sem, vmem_limit_bytes=VMEM_LIMIT)


def _even_body(x_ref, ng_ref, win_ref, bin_ref, vg_ref, vb_ref, ws_ref, bsx_ref, cw_ref, cb_ref,
               cg_ref, cbt_ref, wout_ref, o_ref, h_ref, z_ref, wsb_ref, hh_ref, y_ref, *, rows):
    @pl.when(pl.program_id(1) == 0)
    def _():
        hh_ref[0:HALO, :] = jnp.zeros((HALO, B_WIDTH), F32)

    t_i = lax.broadcasted_iota(jnp.int32, (CHUNK, 2 * CHUNK), 0)
    s_i = lax.broadcasted_iota(jnp.int32, (CHUNK, 2 * CHUNK), 1) & (CHUNK - 1)
    for p in range(A_WIDTH // LANES):
        wsb_ref[p] = jnp.where(s_i <= t_i, ws_ref[p], 0.0).astype(BF16)

    for c in range(rows // CHUNK):
        r = slice(c * CHUNK, (c + 1) * CHUNK)
        h_ref[r, :] = _rms(x_ref[0, r, :], ng_ref[...]).astype(BF16)

    for rb in range(rows // MM_ROWS):
        r = slice(rb * MM_ROWS, (rb + 1) * MM_ROWS)
        for cb in range(IN_WIDTH // MXU_DIM):
            cs = slice(cb * MXU_DIM, (cb + 1) * MXU_DIM)
            z_ref[r, cs] = _dot(h_ref[r, :], win_ref[:, cs]) + bin_ref[:, cs]

    left = lax.broadcasted_iota(jnp.int32, (CHUNK, LANES), 1) < HEAD_DIM
    for c in range(rows // CHUNK):
        r = slice(c * CHUNK, (c + 1) * CHUNK)
        v = jax.nn.gelu(z_ref[r, A_WIDTH:2 * A_WIDTH])
        vn = _layer_norm(v, vg_ref[...], vb_ref[...])
        for p in range(A_WIDTH // LANES):
            ls = slice(p * LANES, (p + 1) * LANES)
            slab = vn[:, ls]
            zero = jnp.zeros_like(slab)
            rhs = jnp.concatenate([jnp.where(left, slab, zero).astype(BF16),
                                   jnp.where(left, zero, slab).astype(BF16)], axis=0)
            mixed = _dot(wsb_ref[p], rhs) + bsx_ref[:, ls]
            u = jax.nn.gelu(z_ref[r, ls])
            y_ref[r, ls] = (u * mixed).astype(BF16)
        a = z_ref[r, 2 * A_WIDTH:2 * A_WIDTH + B_WIDTH]
        g = z_ref[r, 2 * A_WIDTH + B_WIDTH:IN_WIDTH]
        hh_ref[HALO + c * CHUNK:HALO + (c + 1) * CHUNK, :] = a * jax.nn.sigmoid(g)

    for c in range(rows // CHUNK):
        r = slice(c * CHUNK, (c + 1) * CHUNK)
        parts = []
        for lg in range(B_WIDTH // LANES):
            ls = slice(lg * LANES, (lg + 1) * LANES)
            acc = jnp.broadcast_to(cb_ref[:, ls], (CHUNK, LANES))
            for k in range(CONV_WIDTH):
                off = HALO + c * CHUNK - (CONV_WIDTH - 1) + k
                acc = acc + cw_ref[k:k + 1, ls] * hh_ref[off:off + CHUNK, ls]
            parts.append(acc)
        conv = jnp.concatenate(parts, axis=1)
        yb = jax.nn.silu(_layer_norm(conv, cg_ref[...], cbt_ref[...]))
        y_ref[r, A_WIDTH:A_WIDTH + B_WIDTH] = yb.astype(BF16)

    hh_ref[0:HALO, :] = hh_ref[rows:rows + HALO, :]

    for rb in range(rows // MM_ROWS):
        r = slice(rb * MM_ROWS, (rb + 1) * MM_ROWS)
        for cb in range(D_MODEL // MXU_DIM):
            cs = slice(cb * MXU_DIM, (cb + 1) * MXU_DIM)
            o_ref[0, r, cs] = x_ref[0, r, cs] + _dot(y_ref[r, :], wout_ref[:, cs])


def _even_mixer(x, norm_g, w_in, b_in, v_ln_g, v_ln_b, w_s, b_s, conv_w, conv_b, conv_ln_g, conv_ln_b,
                w_out):
    bn, s, _ = x.shape
    rows = ROW_TILE
    n_pairs = A_WIDTH // LANES
    ws_pair = w_s.reshape(n_pairs, 2, CHUNK, CHUNK).transpose(0, 2, 1, 3).reshape(n_pairs, CHUNK, 2 * CHUNK)
    bsx = jnp.repeat(b_s.T, HEAD_DIM, axis=1)
    row = lambda a: a.reshape(1, -1)
    tile = pl.BlockSpec((1, rows, D_MODEL), lambda b, i: (b, i, 0))
    return pl.pallas_call(
        functools.partial(_even_body, rows=rows),
        grid=(bn, s // rows),
        in_specs=[tile, _const_spec((1, D_MODEL)), _const_spec((D_MODEL, IN_WIDTH)), _const_spec((1, IN_WIDTH)),
                  _const_spec((1, A_WIDTH)), _const_spec((1, A_WIDTH)),
                  _const_spec((n_pairs, CHUNK, 2 * CHUNK)), _const_spec((CHUNK, A_WIDTH)),
                  _const_spec((CONV_WIDTH, B_WIDTH)), _const_spec((1, B_WIDTH)),
                  _const_spec((1, B_WIDTH)), _const_spec((1, B_WIDTH)),
                  _const_spec((A_WIDTH + B_WIDTH, D_MODEL))],
        out_specs=tile,
        out_shape=jax.ShapeDtypeStruct(x.shape, F32),
        scratch_shapes=[pltpu.VMEM((rows, D_MODEL), BF16), pltpu.VMEM((rows, IN_WIDTH), F32),
                        pltpu.VMEM((n_pairs, CHUNK, 2 * CHUNK), BF16),
                        pltpu.VMEM((HALO + rows, B_WIDTH), F32),
                        pltpu.VMEM((rows, A_WIDTH + B_WIDTH), BF16)],
        compiler_params=_params(("arbitrary", "arbitrary")),
        name="even_mixer",
    )(x, row(norm_g), w_in.astype(BF16), row(b_in), row(v_ln_g), row(v_ln_b), ws_pair, bsx, conv_w,
      row(conv_b), row(conv_ln_g), row(conv_ln_b), w_out.astype(BF16))


def _ffn_body(x_ref, ng_ref, wup_ref, cw_ref, cb_ref, wdn_ref, fg_ref, o_ref, h_ref, u_ref, carry_ref,
              act_ref, *, rows, final_norm):
    n_chunks = D_FF // FF_CHUNK

    @pl.when(pl.program_id(1) == 0)
    def _():
        carry_ref[...] = jnp.zeros(carry_ref.shape, F32)

    for c in range(rows // CHUNK):
        r = slice(c * CHUNK, (c + 1) * CHUNK)
        h_ref[r, :] = _rms(x_ref[0, r, :], ng_ref[...]).astype(BF16)

    for j in range(n_chunks):
        cols = (slice(j * FF_CHUNK, (j + 1) * FF_CHUNK), slice(D_FF + j * FF_CHUNK, D_FF + (j + 1) * FF_CHUNK))
        ccols = slice(2 * j * FF_CHUNK, 2 * (j + 1) * FF_CHUNK)
        u_ref[0:SUBLANES, :] = carry_ref[:, ccols]
        for rb in range(rows // MM_ROWS):
            r = slice(rb * MM_ROWS, (rb + 1) * MM_ROWS)
            ur = slice(SUBLANES + rb * MM_ROWS, SUBLANES + (rb + 1) * MM_ROWS)
            for half in range(2):
                u_ref[ur, half * FF_CHUNK:(half + 1) * FF_CHUNK] = _dot(h_ref[r, :], wup_ref[:, cols[half]])
        carry_ref[:, ccols] = u_ref[rows:rows + SUBLANES, :]
        for c in range(rows // CHUNK):
            conv = []
            for half in range(2):
                us = slice(half * FF_CHUNK, (half + 1) * FF_CHUNK)
                acc = jnp.broadcast_to(cb_ref[:, cols[half]], (CHUNK, FF_CHUNK))
                for k in range(FFN_CONV_WIDTH):
                    off = SUBLANES + c * CHUNK - (FFN_CONV_WIDTH - 1) + k
                    acc = acc + cw_ref[k:k + 1, cols[half]] * u_ref[off:off + CHUNK, us]
                conv.append(acc)
            act_ref[c * CHUNK:(c + 1) * CHUNK, cols[0]] = (jax.nn.silu(conv[0]) * conv[1]).astype(BF16)

    for rb in range(rows // MM_ROWS):
        r = slice(rb * MM_ROWS, (rb + 1) * MM_ROWS)
        for cb in range(D_MODEL // MXU_DIM):
            cs = slice(cb * MXU_DIM, (cb + 1) * MXU_DIM)
            o_ref[0, r, cs] = x_ref[0, r, cs] + _dot(act_ref[r, :], wdn_ref[:, cs])
        if final_norm:
            o_ref[0, r, :] = _rms(o_ref[0, r, :], fg_ref[...])


def _conv_ffn(x, norm_g, w_up, conv_w, conv_b, w_down, final_g, *, final_norm):
    bn, s, _ = x.shape
    rows = ROW_TILE
    row = lambda a: a.reshape(1, -1)
    tile = pl.BlockSpec((1, rows, D_MODEL), lambda b, i: (b, i, 0))
    return pl.pallas_call(
        functools.partial(_ffn_body, rows=rows, final_norm=final_norm),
        grid=(bn, s // rows),
        in_specs=[tile, _const_spec((1, D_MODEL)), _const_spec((D_MODEL, 2 * D_FF)),
                  _const_spec((FFN_CONV_WIDTH, 2 * D_FF)), _const_spec((1, 2 * D_FF)),
                  _const_spec((D_FF, D_MODEL)), _const_spec((1, D_MODEL))],
        out_specs=tile,
        out_shape=jax.ShapeDtypeStruct(x.shape, F32),
        scratch_shapes=[pltpu.VMEM((rows, D_MODEL), BF16),
                        pltpu.VMEM((SUBLANES + rows, 2 * FF_CHUNK), F32),
                        pltpu.VMEM((SUBLANES, 2 * D_FF), F32),
                        pltpu.VMEM((rows, D_FF), BF16)],
        compiler_params=_params(("arbitrary", "arbitrary")),
        name="conv_ffn_final" if final_norm else "conv_ffn",
    )(x, row(norm_g), w_up.astype(BF16), conv_w, row(conv_b), w_down.astype(BF16), row(final_g))


def _qkv_body(x_ref, ng_ref, w_ref, cos_ref, sin_ref, q_ref, k_ref, v_ref, h_ref, *, rows):
    for c in range(rows // CHUNK):
        r = slice(c * CHUNK, (c + 1) * CHUNK)
        h_ref[r, :] = _rms(x_ref[0, r, :], ng_ref[...]).astype(BF16)

    half = HEAD_DIM // 2
    first_half = (lax.broadcasted_iota(jnp.int32, (MM_ROWS, MXU_DIM), 1) & (HEAD_DIM - 1)) < half
    for rb in range(rows // MM_ROWS):
        r = slice(rb * MM_ROWS, (rb + 1) * MM_ROWS)
        cos = jnp.concatenate([cos_ref[r, :]] * (MXU_DIM // LANES), axis=1)
        sin = jnp.concatenate([sin_ref[r, :]] * (MXU_DIM // LANES), axis=1)
        for which, out_ref in enumerate((q_ref, k_ref, v_ref)):
            for cb in range(D_MODEL // MXU_DIM):
                cs = slice(cb * MXU_DIM, (cb + 1) * MXU_DIM)
                ws = slice(which * D_MODEL + cb * MXU_DIM, which * D_MODEL + (cb + 1) * MXU_DIM)
                z = _dot(h_ref[r, :], w_ref[:, ws])
                if which < 2:
                    partner = jnp.where(first_half, pltpu.roll(z, MXU_DIM - half, 1), pltpu.roll(z, half, 1))
                    z = z * cos + partner * sin
                if which == 0:
                    z = z * (HEAD_DIM ** -0.5)
                out_ref[0, r, cs] = z.astype(BF16)


def _qkv_rotary(x, norm_g, w_qkv):
    bn, s, _ = x.shape
    rows = ROW_TILE
    half = HEAD_DIM // 2
    inv = ROPE_THETA ** (-jnp.arange(half, dtype=F32) / half)
    ang = jnp.arange(s).astype(F32)[:, None] * inv[None, :]
    cos = jnp.tile(jnp.cos(ang), (1, LANES // half))
    sin = jnp.tile(jnp.concatenate([-jnp.sin(ang), jnp.sin(ang)], axis=1), (1, LANES // HEAD_DIM))
    tile = pl.BlockSpec((1, rows, D_MODEL), lambda b, i: (b, i, 0))
    tab = pl.BlockSpec((rows, LANES), lambda b, i: (i, 0))
    out = jax.ShapeDtypeStruct(x.shape, BF16)
    return pl.pallas_call(
        functools.partial(_qkv_body, rows=rows),
        grid=(bn, s // rows),
        in_specs=[tile, _const_spec((1, D_MODEL)), _const_spec((D_MODEL, 3 * D_MODEL)), tab, tab],
        out_specs=[tile, tile, tile],
        out_shape=[out, out, out],
        scratch_shapes=[pltpu.VMEM((rows, D_MODEL), BF16)],
        compiler_params=_params(("arbitrary", "arbitrary")),
        name="qkv_rotary",
    )(x, norm_g.reshape(1, -1), w_qkv.astype(BF16), cos, sin)


def _attn_body(q_ref, kp_ref, kc_ref, vp_ref, vc_ref, o_ref, qf_ref, kf_ref, vf_ref, acc_ref, m_ref, l_ref,
               of_ref, *, tile):
    tile_idx = pl.program_id(1)
    qf_ref[...] = q_ref[0].astype(F32)
    kf_ref[0:tile, :] = kp_ref[0].astype(F32)
    kf_ref[tile:2 * tile, :] = kc_ref[0].astype(F32)
    vf_ref[0:tile, :] = vp_ref[0].astype(F32)
    vf_ref[tile:2 * tile, :] = vc_ref[0].astype(F32)

    qi = lax.broadcasted_iota(jnp.int32, (BLOCK, 2 * BLOCK), 0)
    kj = lax.broadcasted_iota(jnp.int32, (BLOCK, 2 * BLOCK), 1)
    band = (kj >= qi) & (kj <= qi + BLOCK)
    left_q = lax.broadcasted_iota(jnp.int32, (BLOCK, LANES), 1) < HEAD_DIM
    left_k = lax.broadcasted_iota(jnp.int32, (2 * BLOCK, LANES), 1) < HEAD_DIM
    n_branch = len(DILATED_PATTERNS)
    blocks_per_tile = tile // BLOCK

    for bi, (window, dil) in enumerate(DILATED_PATTERNS):
        assert window // dil == BLOCK
        span = dil * BLOCK
        first, last = bi == 0, bi == n_branch - 1

        def block(n, carry, dil=dil, span=span, first=first, last=last):
            res = n % dil
            sb = n // dil
            q0 = sb * span + res
            k0 = tile + q0 - span
            if dil == 1:
                qidx = pl.ds(pl.multiple_of(q0, BLOCK), BLOCK)
                kidx = pl.ds(pl.multiple_of(k0, BLOCK), 2 * BLOCK)
            else:
                qidx = pl.ds(q0, BLOCK, stride=dil)
                kidx = pl.ds(k0, 2 * BLOCK, stride=dil)
            lo = jnp.where((tile_idx == 0) & (sb == 0), BLOCK, 0)
            valid = band & (kj >= lo)

            q = qf_ref[qidx, :]
            k = kf_ref[kidx, :].astype(BF16)
            v = vf_ref[kidx, :]
            zq = jnp.zeros_like(q)
            zv = jnp.zeros_like(v)
            qh = (jnp.where(left_q, q, zq).astype(BF16), jnp.where(left_q, zq, q).astype(BF16))
            vh = (jnp.where(left_k, v, zv).astype(BF16), jnp.where(left_k, zv, v).astype(BF16))
            pv, alpha, l_new = [], [], []
            for h in range(2):
                s = jnp.where(valid, _dot_nt(qh[h], k), NEG)
                sa, sb_ = s[:, :LANES], s[:, LANES:]
                mx = jnp.broadcast_to(jnp.max(jnp.maximum(sa, sb_), axis=-1, keepdims=True), (BLOCK, LANES))
                if first:
                    m_new = mx
                else:
                    m_old = m_ref[h, qidx, :]
                    m_new = jnp.maximum(m_old, mx)
                pa = jnp.exp(sa - m_new)
                pb = jnp.exp(sb_ - m_new)
                rs = jnp.broadcast_to(jnp.sum(pa + pb, axis=-1, keepdims=True), (BLOCK, LANES))
                p = jnp.concatenate([pa, pb], axis=1).astype(BF16)
                pv.append(_dot(p, vh[h]))
                if first:
                    l_new.append(rs)
                else:
                    a = jnp.exp(m_old - m_new)
                    alpha.append(a)
                    l_new.append(a * l_ref[h, qidx, :] + rs)
                if not last:
                    m_ref[h, qidx, :] = m_new
                    l_ref[h, qidx, :] = l_new[h]
            acc = pv[0] + pv[1]
            if not first:
                acc = acc + acc_ref[qidx, :] * jnp.where(left_q, alpha[0], alpha[1])
            if last:
                of_ref[qidx, :] = acc / jnp.where(left_q, l_new[0], l_new[1])
            else:
                acc_ref[qidx, :] = acc
            return carry

        lax.fori_loop(0, blocks_per_tile, block, 0)

    o_ref[0] = of_ref[...].astype(BF16)


def _dilated_attention(q, k, v):
    bn, s, _ = q.shape
    tile = ATTN_TILE
    n_pairs = D_MODEL // LANES
    cur = pl.BlockSpec((1, tile, LANES), lambda b, i, p: (b, i, p))
    prev = pl.BlockSpec((1, tile, LANES), lambda b, i, p: (b, jnp.maximum(i - 1, 0), p))
    return pl.pallas_call(
        functools.partial(_attn_body, tile=tile),
        grid=(bn, s // tile, n_pairs),
        in_specs=[cur, prev, cur, prev, cur],
        out_specs=cur,
        out_shape=jax.ShapeDtypeStruct(q.shape, BF16),
        scratch_shapes=[pltpu.VMEM((tile, LANES), F32), pltpu.VMEM((2 * tile, LANES), F32),
                        pltpu.VMEM((2 * tile, LANES), F32), pltpu.VMEM((tile, LANES), F32),
                        pltpu.VMEM((2, tile, LANES), F32), pltpu.VMEM((2, tile, LANES), F32),
                        pltpu.VMEM((tile, LANES), F32)],
        compiler_params=_params(("arbitrary", "arbitrary", "arbitrary")),
        name="dilated_attention",
    )(q, k, k, v, v)


def _oproj_body(x_ref, a_ref, w_ref, o_ref, *, rows):
    for rb in range(rows // MM_ROWS):
        r = slice(rb * MM_ROWS, (rb + 1) * MM_ROWS)
        for cb in range(D_MODEL // MXU_DIM):
            cs = slice(cb * MXU_DIM, (cb + 1) * MXU_DIM)
            o_ref[0, r, cs] = x_ref[0, r, cs] + _dot(a_ref[0, r, :], w_ref[:, cs])


def _attn_out_proj(x, a, w_o):
    bn, s, _ = x.shape
    rows = ROW_TILE
    tile = pl.BlockSpec((1, rows, D_MODEL), lambda b, i: (b, i, 0))
    return pl.pallas_call(
        functools.partial(_oproj_body, rows=rows),
        grid=(bn, s // rows),
        in_specs=[tile, tile, _const_spec((D_MODEL, D_MODEL))],
        out_specs=tile,
        out_shape=jax.ShapeDtypeStruct(x.shape, F32),
        compiler_params=_params(("arbitrary", "arbitrary")),
        name="attn_out_proj",
    )(x, a, w_o.astype(BF16))


def kernel(x, even_norm_g, even_w_in, even_b_in, even_v_ln_g, even_v_ln_b, even_w_s, even_b_s, even_conv_w,
           even_conv_b, even_conv_ln_g, even_conv_ln_b, even_w_out, odd_norm_g, odd_w_qkv, odd_w_o, ffn_norm_g,
           ffn_w_up, ffn_conv_w, ffn_conv_b, ffn_w_down, final_norm_g):
    depth = ffn_norm_g.shape[0]
    for i in range(depth):
        j = i // 2
        if i % 2 == 0:
            x = _even_mixer(x, even_norm_g[j], even_w_in[j], even_b_in[j], even_v_ln_g[j], even_v_ln_b[j],
                            even_w_s[j], even_b_s[j], even_conv_w[j], even_conv_b[j], even_conv_ln_g[j],
                            even_conv_ln_b[j], even_w_out[j])
        else:
            q, k, v = _qkv_rotary(x, odd_norm_g[j], odd_w_qkv[j])
            x = _attn_out_proj(x, _dilated_attention(q, k, v), odd_w_o[j])
        x = _conv_ffn(x, ffn_norm_g[i], ffn_w_up[i], ffn_conv_w[i], ffn_conv_b[i], ffn_w_down[i],
                      final_norm_g, final_norm=(i == depth - 1))
    return x
```

```python
import functools
import math

import jax
import jax.numpy as jnp
from jax import lax
from jax.experimental import pallas as pl
from jax.experimental.pallas import tpu as pltpu

F32 = jnp.float32
BF16 = jnp.bfloat16

D_MODEL = 1024
HEAD_DIM = 64
A_WIDTH = 512
CHUNK = 128
B_WIDTH = 512
CONV_WIDTH = 31
IN_WIDTH = 2 * A_WIDTH + 2 * B_WIDTH
DILATED_PATTERNS = ((128, 1), (512, 4), (2048, 16))
BLOCK = 128
ROPE_THETA = 10000.0
D_FF = 2816
FFN_CONV_WIDTH = 3
EPS = 1e-6
NEG = -1e30

LANES = 128
SUBLANES = 8
MXU_DIM = 256
ROW_TILE = 512
FFN_ROW_TILE = 1024
MM_ROWS = 256
HALO = 32
FF_CHUNK = 256
N_PAIRS = D_MODEL // LANES
MAX_DIL = max(d for _, d in DILATED_PATTERNS)
ATTN_TILE = MAX_DIL * BLOCK
VMEM_LIMIT = 56 * 1024 * 1024


def _dot(a, b):
    return jnp.dot(a, b, preferred_element_type=F32)


def _dot_nt(a, b):
    return lax.dot_general(a, b, (((1,), (1,)), ((), ())), preferred_element_type=F32)


def _rms(x, g):
    ms = jnp.mean(x * x, axis=-1, keepdims=True)
    return x * lax.rsqrt(ms + EPS) * g


def _layer_norm(x, g, b):
    mu = jnp.mean(x, axis=-1, keepdims=True)
    xc = x - mu
    var = jnp.mean(xc * xc, axis=-1, keepdims=True)
    return xc * lax.rsqrt(var + EPS) * g + b


def _const_spec(shape):
    return pl.BlockSpec(shape, lambda *_: (0,) * len(shape), pipeline_mode=pl.Buffered(1))


def _params(sem):
    return pltpu.CompilerParams(dimension_semantics=sem, vmem_limit_bytes=VMEM_LIMIT)


def _even_body(x_ref, ng_ref, win_ref, bin_ref, vg_ref, vb_ref, ws_ref, bsx_ref, cw_ref, cb_ref,
               cg_ref, cbt_ref, wout_ref, o_ref, h_ref, z_ref, wsb_ref, hh_ref, y_ref, *, rows):
    @pl.when(pl.program_id(1) == 0)
    def _():
        hh_ref[0:HALO, :] = jnp.zeros((HALO, B_WIDTH), F32)

    t_i = lax.broadcasted_iota(jnp.int32, (CHUNK, 2 * CHUNK), 0)
    s_i = lax.broadcasted_iota(jnp.int32, (CHUNK, 2 * CHUNK), 1) & (CHUNK - 1)
    for p in range(A_WIDTH // LANES):
        wsb_ref[p] = jnp.where(s_i <= t_i, ws_ref[p], 0.0).astype(BF16)

    for c in range(rows // CHUNK):
        r = slice(c * CHUNK, (c + 1) * CHUNK)
        h_ref[r, :] = _rms(x_ref[0, r, :], ng_ref[...]).astype(BF16)

    for rb in range(rows // MM_ROWS):
        r = slice(rb * MM_ROWS, (rb + 1) * MM_ROWS)
        for cb in range(IN_WIDTH // MXU_DIM):
            cs = slice(cb * MXU_DIM, (cb + 1) * MXU_DIM)
            z_ref[r, cs] = _dot(h_ref[r, :], win_ref[:, cs]) + bin_ref[:, cs]

    left = lax.broadcasted_iota(jnp.int32, (CHUNK, LANES), 1) < HEAD_DIM
    for c in range(rows // CHUNK):
        r = slice(c * CHUNK, (c + 1) * CHUNK)
        v = jax.nn.gelu(z_ref[r, A_WIDTH:2 * A_WIDTH])
        vn = _layer_norm(v, vg_ref[...], vb_ref[...])
        for p in range(A_WIDTH // LANES):
            ls = slice(p * LANES, (p + 1) * LANES)
            slab = vn[:, ls]
            zero = jnp.zeros_like(slab)
            rhs = jnp.concatenate([jnp.where(left, slab, zero).astype(BF16),
                                   jnp.where(left, zero, slab).astype(BF16)], axis=0)
            mixed = _dot(wsb_ref[p], rhs) + bsx_ref[:, ls]
            u = jax.nn.gelu(z_ref[r, ls])
            y_ref[r, ls] = (u * mixed).astype(BF16)
        a = z_ref[r, 2 * A_WIDTH:2 * A_WIDTH + B_WIDTH]
        g = z_ref[r, 2 * A_WIDTH + B_WIDTH:IN_WIDTH]
        hh_ref[HALO + c * CHUNK:HALO + (c + 1) * CHUNK, :] = a * jax.nn.sigmoid(g)

    for c in range(rows // CHUNK):
        r = slice(c * CHUNK, (c + 1) * CHUNK)
        parts = []
        for lg in range(B_WIDTH // LANES):
            ls = slice(lg * LANES, (lg + 1) * LANES)
            window = hh_ref[c * CHUNK:c * CHUNK + HALO + CHUNK, ls]
            rolled = [window] + [pltpu.roll(window, SUBLANES - s, 0) for s in range(1, SUBLANES)]
            acc = jnp.broadcast_to(cb_ref[:, ls], (CHUNK, LANES))
            for k in range(CONV_WIDTH):
                off = HALO - (CONV_WIDTH - 1) + k
                s = off % SUBLANES
                start = off - s + (SUBLANES if s else 0)
                acc = acc + cw_ref[k:k + 1, ls] * rolled[s][start:start + CHUNK]
            parts.append(acc)
        conv = jnp.concatenate(parts, axis=1)
        yb = jax.nn.silu(_layer_norm(conv, cg_ref[...], cbt_ref[...]))
        y_ref[r, A_WIDTH:A_WIDTH + B_WIDTH] = yb.astype(BF16)

    hh_ref[0:HALO, :] = hh_ref[rows:rows + HALO, :]

    for rb in range(rows // MM_ROWS):
        r = slice(rb * MM_ROWS, (rb + 1) * MM_ROWS)
        for cb in range(D_MODEL // MXU_DIM):
            cs = slice(cb * MXU_DIM, (cb + 1) * MXU_DIM)
            o_ref[0, r, cs] = x_ref[0, r, cs] + _dot(y_ref[r, :], wout_ref[:, cs])


def _even_mixer(x, norm_g, w_in, b_in, v_ln_g, v_ln_b, w_s, b_s, conv_w, conv_b, conv_ln_g, conv_ln_b,
                w_out):
    bn, s, _ = x.shape
    rows = ROW_TILE
    n_pairs = A_WIDTH // LANES
    ws_pair = w_s.reshape(n_pairs, 2, CHUNK, CHUNK).transpose(0, 2, 1, 3).reshape(n_pairs, CHUNK, 2 * CHUNK)
    bsx = jnp.repeat(b_s.T, HEAD_DIM, axis=1)
    row = lambda a: a.reshape(1, -1)
    tile = pl.BlockSpec((1, rows, D_MODEL), lambda b, i: (b, i, 0))
    return pl.pallas_call(
        functools.partial(_even_body, rows=rows),
        grid=(bn, s // rows),
        in_specs=[tile, _const_spec((1, D_MODEL)), _const_spec((D_MODEL, IN_WIDTH)), _const_spec((1, IN_WIDTH)),
                  _const_spec((1, A_WIDTH)), _const_spec((1, A_WIDTH)),
                  _const_spec((n_pairs, CHUNK, 2 * CHUNK)), _const_spec((CHUNK, A_WIDTH)),
                  _const_spec((CONV_WIDTH, B_WIDTH)), _const_spec((1, B_WIDTH)),
                  _const_spec((1, B_WIDTH)), _const_spec((1, B_WIDTH)),
                  _const_spec((A_WIDTH + B_WIDTH, D_MODEL))],
        out_specs=tile,
        out_shape=jax.ShapeDtypeStruct(x.shape, F32),
        scratch_shapes=[pltpu.VMEM((rows, D_MODEL), BF16), pltpu.VMEM((rows, IN_WIDTH), F32),
                        pltpu.VMEM((n_pairs, CHUNK, 2 * CHUNK), BF16),
                        pltpu.VMEM((HALO + rows, B_WIDTH), F32),
                        pltpu.VMEM((rows, A_WIDTH + B_WIDTH), BF16)],
        compiler_params=_params(("arbitrary", "arbitrary")),
        name="even_mixer",
    )(x, row(norm_g), w_in.astype(BF16), row(b_in), row(v_ln_g), row(v_ln_b), ws_pair, bsx, conv_w,
      row(conv_b), row(conv_ln_g), row(conv_ln_b), w_out.astype(BF16))


def _ffn_body(x_ref, ng_ref, wup_ref, cw_ref, cb_ref, wdn_ref, fg_ref, o_ref, h_ref, u_ref, carry_ref,
              act_ref, *, rows, final_norm):
    n_chunks = D_FF // FF_CHUNK

    @pl.when(pl.program_id(1) == 0)
    def _():
        carry_ref[...] = jnp.zeros(carry_ref.shape, F32)

    for c in range(rows // CHUNK):
        r = slice(c * CHUNK, (c + 1) * CHUNK)
        h_ref[r, :] = _rms(x_ref[0, r, :], ng_ref[...]).astype(BF16)

    for rb in range(rows // MM_ROWS):
        r = slice(rb * MM_ROWS, (rb + 1) * MM_ROWS)
        for j in range(n_chunks):
            cols = (slice(j * FF_CHUNK, (j + 1) * FF_CHUNK),
                    slice(D_FF + j * FF_CHUNK, D_FF + (j + 1) * FF_CHUNK))
            ccols = slice(2 * j * FF_CHUNK, 2 * (j + 1) * FF_CHUNK)
            buf = u_ref.at[(rb * n_chunks + j) % 2]
            buf[0:SUBLANES, :] = carry_ref[:, ccols]
            for half in range(2):
                buf[SUBLANES:SUBLANES + MM_ROWS, half * FF_CHUNK:(half + 1) * FF_CHUNK] = _dot(
                    h_ref[r, :], wup_ref[:, cols[half]])
            carry_ref[:, ccols] = buf[MM_ROWS:MM_ROWS + SUBLANES, :]
            for c in range(MM_ROWS // CHUNK):
                window = buf[c * CHUNK:c * CHUNK + SUBLANES + CHUNK, :]
                taps = [pltpu.roll(window, FFN_CONV_WIDTH - 1 - k, 0)[SUBLANES:] for k in range(FFN_CONV_WIDTH - 1)]
                taps.append(window[SUBLANES:])
                conv = []
                for half in range(2):
                    us = slice(half * FF_CHUNK, (half + 1) * FF_CHUNK)
                    acc = jnp.broadcast_to(cb_ref[:, cols[half]], (CHUNK, FF_CHUNK))
                    for k in range(FFN_CONV_WIDTH):
                        acc = acc + cw_ref[k:k + 1, cols[half]] * taps[k][:, us]
                    conv.append(acc)
                ar = slice(rb * MM_ROWS + c * CHUNK, rb * MM_ROWS + (c + 1) * CHUNK)
                act_ref[ar, cols[0]] = (jax.nn.silu(conv[0]) * conv[1]).astype(BF16)

        for cb in range(D_MODEL // MXU_DIM):
            cs = slice(cb * MXU_DIM, (cb + 1) * MXU_DIM)
            o_ref[0, r, cs] = x_ref[0, r, cs] + _dot(act_ref[r, :], wdn_ref[:, cs])
        if final_norm:
            o_ref[0, r, :] = _rms(o_ref[0, r, :], fg_ref[...])


def _conv_ffn(x, norm_g, w_up, conv_w, conv_b, w_down, final_g, *, final_norm):
    bn, s, _ = x.shape
    rows = FFN_ROW_TILE
    row = lambda a: a.reshape(1, -1)
    tile = pl.BlockSpec((1, rows, D_MODEL), lambda b, i: (b, i, 0))
    return pl.pallas_call(
        functools.partial(_ffn_body, rows=rows, final_norm=final_norm),
        grid=(bn, s // rows),
        in_specs=[tile, _const_spec((1, D_MODEL)), _const_spec((D_MODEL, 2 * D_FF)),
                  _const_spec((FFN_CONV_WIDTH, 2 * D_FF)), _const_spec((1, 2 * D_FF)),
                  _const_spec((D_FF, D_MODEL)), _const_spec((1, D_MODEL))],
        out_specs=tile,
        out_shape=jax.ShapeDtypeStruct(x.shape, F32),
        scratch_shapes=[pltpu.VMEM((rows, D_MODEL), BF16),
                        pltpu.VMEM((2, SUBLANES + MM_ROWS, 2 * FF_CHUNK), F32),
                        pltpu.VMEM((SUBLANES, 2 * D_FF), F32),
                        pltpu.VMEM((rows, D_FF), BF16)],
        compiler_params=_params(("arbitrary", "arbitrary")),
        name="conv_ffn_final" if final_norm else "conv_ffn",
    )(x, row(norm_g), w_up.astype(BF16), conv_w, row(conv_b), w_down.astype(BF16), row(final_g))


def _qkv_body(x_ref, ng_ref, w_ref, cos_ref, sin_ref, q_ref, k_ref, v_ref, h_ref, *, rows):
    for c in range(rows // CHUNK):
        r = slice(c * CHUNK, (c + 1) * CHUNK)
        h_ref[r, :] = _rms(x_ref[0, r, :], ng_ref[...]).astype(BF16)

    half = HEAD_DIM // 2
    first_half = (lax.broadcasted_iota(jnp.int32, (MM_ROWS, MXU_DIM), 1) & (HEAD_DIM - 1)) < half
    q_scale = (HEAD_DIM ** -0.5) * math.log2(math.e)
    for rb in range(rows // MM_ROWS):
        r = slice(rb * MM_ROWS, (rb + 1) * MM_ROWS)
        cos = jnp.concatenate([cos_ref[r, :]] * (MXU_DIM // LANES), axis=1)
        sin = jnp.concatenate([sin_ref[r, :]] * (MXU_DIM // LANES), axis=1)
        for which, out_ref in enumerate((q_ref, k_ref, v_ref)):
            for cb in range(D_MODEL // MXU_DIM):
                ws = slice(which * D_MODEL + cb * MXU_DIM, which * D_MODEL + (cb + 1) * MXU_DIM)
                z = _dot(h_ref[r, :], w_ref[:, ws])
                if which < 2:
                    partner = jnp.where(first_half, pltpu.roll(z, MXU_DIM - half, 1), pltpu.roll(z, half, 1))
                    z = z * cos + partner * sin
                if which == 0:
                    z = z * q_scale
                for sub in range(MXU_DIM // LANES):
                    pair = cb * (MXU_DIM // LANES) + sub
                    out_ref[0, pair, r, :] = z[:, sub * LANES:(sub + 1) * LANES].astype(BF16)


def _qkv_rotary(x, norm_g, w_qkv):
    bn, s, _ = x.shape
    rows = ROW_TILE
    half = HEAD_DIM // 2
    inv = ROPE_THETA ** (-jnp.arange(half, dtype=F32) / half)
    ang = jnp.arange(s).astype(F32)[:, None] * inv[None, :]
    cos = jnp.tile(jnp.cos(ang), (1, LANES // half))
    sin = jnp.tile(jnp.concatenate([-jnp.sin(ang), jnp.sin(ang)], axis=1), (1, LANES // HEAD_DIM))
    tile = pl.BlockSpec((1, rows, D_MODEL), lambda b, i: (b, i, 0))
    tab = pl.BlockSpec((rows, LANES), lambda b, i: (i, 0))
    out_spec = pl.BlockSpec((1, N_PAIRS, rows, LANES), lambda b, i: (b, 0, i, 0))
    out = jax.ShapeDtypeStruct((bn, N_PAIRS, s, LANES), BF16)
    return pl.pallas_call(
        functools.partial(_qkv_body, rows=rows),
        grid=(bn, s // rows),
        in_specs=[tile, _const_spec((1, D_MODEL)), _const_spec((D_MODEL, 3 * D_MODEL)), tab, tab],
        out_specs=[out_spec, out_spec, out_spec],
        out_shape=[out, out, out],
        scratch_shapes=[pltpu.VMEM((rows, D_MODEL), BF16)],
        compiler_params=_params(("arbitrary", "arbitrary")),
        name="qkv_rotary",
    )(x, norm_g.reshape(1, -1), w_qkv.astype(BF16), cos, sin)


def _query_chunks(dil, res, n):
    per = MAX_DIL // dil
    rows = BLOCK // per
    return [(res + dil * a, rows * n, rows) for a in range(per)]


def _gather(ref, chunks, heads):
    parts = [ref[r, h * BLOCK + row0:h * BLOCK + row0 + n, :] for h in range(heads) for r, row0, n in chunks]
    return parts[0] if len(parts) == 1 else jnp.concatenate(parts, axis=0)


def _scatter(ref, chunks, heads, val):
    off = 0
    for h in range(heads):
        for r, row0, n in chunks:
            ref[r, h * BLOCK + row0:h * BLOCK + row0 + n, :] = val[off:off + n]
            off += n


def _attn_body(*refs):
    n_branch = len(DILATED_PATTERNS)
    q_ref = refs[0]
    kv_refs = refs[1:1 + 4 * n_branch]
    o_ref = refs[1 + 4 * n_branch]
    qs_ref, acc_ref, m_ref, l_ref, mask_ref = refs[2 + 4 * n_branch:]
    first_tile = pl.program_id(1) == 0

    left = lax.broadcasted_iota(jnp.int32, (BLOCK, LANES), 1) < HEAD_DIM
    for r in range(MAX_DIL):
        q = q_ref[0, 0, :, r * LANES:(r + 1) * LANES].astype(F32)
        qs_ref[r, 0:BLOCK, :] = jnp.where(left, q, 0.0)
        qs_ref[r, BLOCK:2 * BLOCK, :] = jnp.where(left, 0.0, q)

    row = lax.broadcasted_iota(jnp.int32, (BLOCK, 2 * BLOCK), 0)
    kj = lax.broadcasted_iota(jnp.int32, (BLOCK, 2 * BLOCK), 1)
    for bi, (window, dil) in enumerate(DILATED_PATTERNS):
        assert window // dil == BLOCK
        per = MAX_DIL // dil
        rows = BLOCK // per
        jq = (row >> (rows.bit_length() - 1)) + per * (row & (rows - 1))
        band = (kj >= jq) & (kj <= jq + BLOCK)
        mask_ref[bi, 0] = band.astype(F32)
        mask_ref[bi, 1] = (band & (kj >= jnp.where(first_tile, BLOCK, 0))).astype(F32)

    for bi, (window, dil) in enumerate(DILATED_PATTERNS):
        kc_ref, kp_ref, vc_ref, vp_ref = kv_refs[4 * bi:4 * bi + 4]
        first, last = bi == 0, bi == n_branch - 1
        assert not last or dil == MAX_DIL
        for res in range(dil):
            ls = slice(res * LANES, (res + 1) * LANES)
            for n in range(MAX_DIL // dil):
                chunks = _query_chunks(dil, res, n)
                if n == 0:
                    k = jnp.concatenate([kp_ref[0, 0, :, ls], kc_ref[0, 0, 0:BLOCK, ls]], axis=0)
                    v = jnp.concatenate([vp_ref[0, 0, :, ls], vc_ref[0, 0, 0:BLOCK, ls]], axis=0)
                else:
                    k = kc_ref[0, 0, (n - 1) * BLOCK:(n + 1) * BLOCK, ls]
                    v = vc_ref[0, 0, (n - 1) * BLOCK:(n + 1) * BLOCK, ls]
                valid = mask_ref[bi, 1 if n == 0 else 0] > 0.5
                valid = jnp.concatenate([valid, valid], axis=0)

                q = _gather(qs_ref, chunks, 2).astype(BF16)
                s = jnp.where(valid, _dot_nt(q, k), NEG)
                sa, sb = s[:, :LANES], s[:, LANES:]
                mx = jnp.broadcast_to(jnp.max(jnp.maximum(sa, sb), axis=-1, keepdims=True), sa.shape)
                if first:
                    m_new = mx
                else:
                    m_old = _gather(m_ref, chunks, 2)
                    m_new = jnp.maximum(m_old, mx)
                pa = jnp.exp2(sa - m_new)
                pb = jnp.exp2(sb - m_new)
                l_new = jnp.broadcast_to(jnp.sum(pa + pb, axis=-1, keepdims=True), sa.shape)
                pv = _dot(jnp.concatenate([pa, pb], axis=1).astype(BF16), v)
                acc = jnp.where(left, pv[:BLOCK], pv[BLOCK:])
                if not first:
                    alpha = jnp.exp2(m_old - m_new)
                    l_new = alpha * _gather(l_ref, chunks, 2) + l_new
                    acc = acc + _gather(acc_ref, chunks, 1) * jnp.where(left, alpha[:BLOCK], alpha[BLOCK:])
                if last:
                    out = acc / jnp.where(left, l_new[:BLOCK], l_new[BLOCK:])
                    o_ref[0, 0, :, ls] = out.astype(BF16)
                else:
                    _scatter(m_ref, chunks, 2, m_new)
                    _scatter(l_ref, chunks, 2, l_new)
                    _scatter(acc_ref, chunks, 1, acc)


def _dilated_attention(q, k, v):
    bn, n_pairs, s, _ = q.shape
    tile = ATTN_TILE

    def view(a, dil):
        return a.reshape(bn, n_pairs, s // dil, dil * LANES)

    def cur_spec(dil):
        return pl.BlockSpec((1, 1, tile // dil, dil * LANES), lambda b, i, p: (b, p, i, 0))

    def prev_spec(dil):
        per = tile // dil // BLOCK
        return pl.BlockSpec((1, 1, BLOCK, dil * LANES), lambda b, i, p: (b, p, jnp.maximum(per * i - 1, 0), 0))

    operands, in_specs = [view(q, MAX_DIL)], [cur_spec(MAX_DIL)]
    for _, dil in DILATED_PATTERNS:
        for a in (k, v):
            operands += [view(a, dil), view(a, dil)]
            in_specs += [cur_spec(dil), prev_spec(dil)]
    out = pl.pallas_call(
        _attn_body,
        grid=(bn, s // tile, n_pairs),
        in_specs=in_specs,
        out_specs=cur_spec(MAX_DIL),
        out_shape=jax.ShapeDtypeStruct((bn, n_pairs, s // MAX_DIL, MAX_DIL * LANES), BF16),
        scratch_shapes=[pltpu.VMEM((MAX_DIL, 2 * BLOCK, LANES), F32), pltpu.VMEM((MAX_DIL, BLOCK, LANES), F32),
                        pltpu.VMEM((MAX_DIL, 2 * BLOCK, LANES), F32), pltpu.VMEM((MAX_DIL, 2 * BLOCK, LANES), F32),
                        pltpu.VMEM((len(DILATED_PATTERNS), 2, BLOCK, 2 * BLOCK), F32)],
        compiler_params=_params(("arbitrary", "arbitrary", "arbitrary")),
        name="dilated_attention",
    )(*operands)
    return out.reshape(bn, n_pairs, s, LANES)


def _oproj_body(x_ref, a_ref, w_ref, o_ref, *, rows):
    for rb in range(rows // MM_ROWS):
        r = slice(rb * MM_ROWS, (rb + 1) * MM_ROWS)
        a = jnp.concatenate([a_ref[0, p, r, :] for p in range(N_PAIRS)], axis=1)
        for cb in range(D_MODEL // MXU_DIM):
            cs = slice(cb * MXU_DIM, (cb + 1) * MXU_DIM)
            o_ref[0, r, cs] = x_ref[0, r, cs] + _dot(a, w_ref[:, cs])


def _attn_out_proj(x, a, w_o):
    bn, s, _ = x.shape
    rows = ROW_TILE
    tile = pl.BlockSpec((1, rows, D_MODEL), lambda b, i: (b, i, 0))
    a_spec = pl.BlockSpec((1, N_PAIRS, rows, LANES), lambda b, i: (b, 0, i, 0))
    return pl.pallas_call(
        functools.partial(_oproj_body, rows=rows),
        grid=(bn, s // rows),
        in_specs=[tile, a_spec, _const_spec((D_MODEL, D_MODEL))],
        out_specs=tile,
        out_shape=jax.ShapeDtypeStruct(x.shape, F32),
        compiler_params=_params(("arbitrary", "arbitrary")),
        name="attn_out_proj",
    )(x, a, w_o.astype(BF16))


def kernel(x, even_norm_g, even_w_in, even_b_in, even_v_ln_g, even_v_ln_b, even_w_s, even_b_s, even_conv_w,
           even_conv_b, even_conv_ln_g, even_conv_ln_b, even_w_out, odd_norm_g, odd_w_qkv, odd_w_o, ffn_norm_g,
           ffn_w_up, ffn_conv_w, ffn_conv_b, ffn_w_down, final_norm_g):
    depth = ffn_norm_g.shape[0]
    for i in range(depth):
        j = i // 2
        if i % 2 == 0:
            x = _even_mixer(x, even_norm_g[j], even_w_in[j], even_b_in[j], even_v_ln_g[j], even_v_ln_b[j],
                            even_w_s[j], even_b_s[j], even_conv_w[j], even_conv_b[j], even_conv_ln_g[j],
                            even_conv_ln_b[j], even_w_out[j])
        else:
            q, k, v = _qkv_rotary(x, odd_norm_g[j], odd_w_qkv[j])
            x = _attn_out_proj(x, _dilated_attention(q, k, v), odd_w_o[j])
        x = _conv_ffn(x, ffn_norm_g[i], ffn_w_up[i], ffn_conv_w[i], ffn_conv_b[i], ffn_w_down[i],
                      final_norm_g, final_norm=(i == depth - 1))
    return x
```

```python
import functools
import math

import jax
import jax.numpy as jnp
from jax import lax
from jax.experimental import pallas as pl
from jax.experimental.pallas import tpu as pltpu

F32 = jnp.float32
BF16 = jnp.bfloat16

D_MODEL = 1024
HEAD_DIM = 64
A_WIDTH = 512
CHUNK = 128
B_WIDTH = 512
CONV_WIDTH = 31
IN_WIDTH = 2 * A_WIDTH + 2 * B_WIDTH
DILATED_PATTERNS = ((128, 1), (512, 4), (2048, 16))
BLOCK = 128
ROPE_THETA = 10000.0
D_FF = 2816
FFN_CONV_WIDTH = 3
EPS = 1e-6
NEG = -1e30

LANES = 128
SUBLANES = 8
MXU_DIM = 256
ROW_TILE = 512
FFN_ROW_TILE = 1024
MM_ROWS = 256
HALO = 32
FF_CHUNK = 256
N_PAIRS = D_MODEL // LANES
DILS = tuple(d for _, d in DILATED_PATTERNS)
MAX_DIL = DILS[-1]
ATTN_TILE = MAX_DIL * BLOCK
VMEM_LIMIT = 56 * 1024 * 1024


def _dot(a, b):
    return jnp.dot(a, b, preferred_element_type=F32)


def _dot_nt(a, b):
    return lax.dot_general(a, b, (((1,), (1,)), ((), ())), preferred_element_type=F32)


def _rms(x, g):
    ms = jnp.mean(x * x, axis=-1, keepdims=True)
    return x * lax.rsqrt(ms + EPS) * g


def _layer_norm(x, g, b):
    mu = jnp.mean(x, axis=-1, keepdims=True)
    xc = x - mu
    var = jnp.mean(xc * xc, axis=-1, keepdims=True)
    return xc * lax.rsqrt(var + EPS) * g + b


def _const_spec(shape):
    return pl.BlockSpec(shape, lambda *_: (0,) * len(shape), pipeline_mode=pl.Buffered(1))


def _params(sem):
    return pltpu.CompilerParams(dimension_semantics=sem, vmem_limit_bytes=VMEM_LIMIT)


def _even_body(x_ref, ng_ref, win_ref, bin_ref, vg_ref, vb_ref, ws_ref, bsx_ref, cw_ref, cb_ref,
               cg_ref, cbt_ref, wout_ref, o_ref, h_ref, z_ref, wsb_ref, hh_ref, y_ref, *, rows):
    @pl.when(pl.program_id(1) == 0)
    def _():
        hh_ref[0:HALO, :] = jnp.zeros((HALO, B_WIDTH), F32)

    t_i = lax.broadcasted_iota(jnp.int32, (CHUNK, 2 * CHUNK), 0)
    s_i = lax.broadcasted_iota(jnp.int32, (CHUNK, 2 * CHUNK), 1) & (CHUNK - 1)
    for p in range(A_WIDTH // LANES):
        wsb_ref[p] = jnp.where(s_i <= t_i, ws_ref[p], 0.0).astype(BF16)

    for c in range(rows // CHUNK):
        r = slice(c * CHUNK, (c + 1) * CHUNK)
        h_ref[r, :] = _rms(x_ref[0, r, :], ng_ref[...]).astype(BF16)

    for rb in range(rows // MM_ROWS):
        r = slice(rb * MM_ROWS, (rb + 1) * MM_ROWS)
        for cb in range(IN_WIDTH // MXU_DIM):
            cs = slice(cb * MXU_DIM, (cb + 1) * MXU_DIM)
            z_ref[r, cs] = _dot(h_ref[r, :], win_ref[:, cs]) + bin_ref[:, cs]

    left = lax.broadcasted_iota(jnp.int32, (CHUNK, LANES), 1) < HEAD_DIM
    for c in range(rows // CHUNK):
        r = slice(c * CHUNK, (c + 1) * CHUNK)
        v = jax.nn.gelu(z_ref[r, A_WIDTH:2 * A_WIDTH])
        vn = _layer_norm(v, vg_ref[...], vb_ref[...])
        for p in range(A_WIDTH // LANES):
            ls = slice(p * LANES, (p + 1) * LANES)
            slab = vn[:, ls]
            zero = jnp.zeros_like(slab)
            rhs = jnp.concatenate([jnp.where(left, slab, zero).astype(BF16),
                                   jnp.where(left, zero, slab).astype(BF16)], axis=0)
            mixed = _dot(wsb_ref[p], rhs) + bsx_ref[:, ls]
            u = jax.nn.gelu(z_ref[r, ls])
            y_ref[r, ls] = (u * mixed).astype(BF16)
        a = z_ref[r, 2 * A_WIDTH:2 * A_WIDTH + B_WIDTH]
        g = z_ref[r, 2 * A_WIDTH + B_WIDTH:IN_WIDTH]
        hh_ref[HALO + c * CHUNK:HALO + (c + 1) * CHUNK, :] = a * jax.nn.sigmoid(g)

    for c in range(rows // CHUNK):
        r = slice(c * CHUNK, (c + 1) * CHUNK)
        parts = []
        for lg in range(B_WIDTH // LANES):
            ls = slice(lg * LANES, (lg + 1) * LANES)
            window = hh_ref[c * CHUNK:c * CHUNK + HALO + CHUNK, ls]
            rolled = [window] + [pltpu.roll(window, SUBLANES - s, 0) for s in range(1, SUBLANES)]
            acc = jnp.broadcast_to(cb_ref[:, ls], (CHUNK, LANES))
            for k in range(CONV_WIDTH):
                off = HALO - (CONV_WIDTH - 1) + k
                s = off % SUBLANES
                start = off - s + (SUBLANES if s else 0)
                acc = acc + cw_ref[k:k + 1, ls] * rolled[s][start:start + CHUNK]
            parts.append(acc)
        conv = jnp.concatenate(parts, axis=1)
        yb = jax.nn.silu(_layer_norm(conv, cg_ref[...], cbt_ref[...]))
        y_ref[r, A_WIDTH:A_WIDTH + B_WIDTH] = yb.astype(BF16)

    hh_ref[0:HALO, :] = hh_ref[rows:rows + HALO, :]

    for rb in range(rows // MM_ROWS):
        r = slice(rb * MM_ROWS, (rb + 1) * MM_ROWS)
        for cb in range(D_MODEL // MXU_DIM):
            cs = slice(cb * MXU_DIM, (cb + 1) * MXU_DIM)
            o_ref[0, r, cs] = x_ref[0, r, cs] + _dot(y_ref[r, :], wout_ref[:, cs])


def _even_mixer(x, norm_g, w_in, b_in, v_ln_g, v_ln_b, w_s, b_s, conv_w, conv_b, conv_ln_g, conv_ln_b,
                w_out):
    bn, s, _ = x.shape
    rows = ROW_TILE
    n_pairs = A_WIDTH // LANES
    ws_pair = w_s.reshape(n_pairs, 2, CHUNK, CHUNK).transpose(0, 2, 1, 3).reshape(n_pairs, CHUNK, 2 * CHUNK)
    bsx = jnp.repeat(b_s.T, HEAD_DIM, axis=1)
    row = lambda a: a.reshape(1, -1)
    tile = pl.BlockSpec((1, rows, D_MODEL), lambda b, i: (b, i, 0))
    return pl.pallas_call(
        functools.partial(_even_body, rows=rows),
        grid=(bn, s // rows),
        in_specs=[tile, _const_spec((1, D_MODEL)), _const_spec((D_MODEL, IN_WIDTH)), _const_spec((1, IN_WIDTH)),
                  _const_spec((1, A_WIDTH)), _const_spec((1, A_WIDTH)),
                  _const_spec((n_pairs, CHUNK, 2 * CHUNK)), _const_spec((CHUNK, A_WIDTH)),
                  _const_spec((CONV_WIDTH, B_WIDTH)), _const_spec((1, B_WIDTH)),
                  _const_spec((1, B_WIDTH)), _const_spec((1, B_WIDTH)),
                  _const_spec((A_WIDTH + B_WIDTH, D_MODEL))],
        out_specs=tile,
        out_shape=jax.ShapeDtypeStruct(x.shape, F32),
        scratch_shapes=[pltpu.VMEM((rows, D_MODEL), BF16), pltpu.VMEM((rows, IN_WIDTH), F32),
                        pltpu.VMEM((n_pairs, CHUNK, 2 * CHUNK), BF16),
                        pltpu.VMEM((HALO + rows, B_WIDTH), F32),
                        pltpu.VMEM((rows, A_WIDTH + B_WIDTH), BF16)],
        compiler_params=_params(("arbitrary", "arbitrary")),
        name="even_mixer",
    )(x, row(norm_g), w_in.astype(BF16), row(b_in), row(v_ln_g), row(v_ln_b), ws_pair, bsx, conv_w,
      row(conv_b), row(conv_ln_g), row(conv_ln_b), w_out.astype(BF16))


def _ffn_body(x_ref, ng_ref, wup_ref, cw_ref, cb_ref, wdn_ref, fg_ref, o_ref, h_ref, u_ref, carry_ref,
              act_ref, *, rows, final_norm):
    n_chunks = D_FF // FF_CHUNK

    @pl.when(pl.program_id(1) == 0)
    def _():
        carry_ref[...] = jnp.zeros(carry_ref.shape, F32)

    for c in range(rows // CHUNK):
        r = slice(c * CHUNK, (c + 1) * CHUNK)
        h_ref[r, :] = _rms(x_ref[0, r, :], ng_ref[...]).astype(BF16)

    for rb in range(rows // MM_ROWS):
        r = slice(rb * MM_ROWS, (rb + 1) * MM_ROWS)
        for j in range(n_chunks):
            cols = (slice(j * FF_CHUNK, (j + 1) * FF_CHUNK),
                    slice(D_FF + j * FF_CHUNK, D_FF + (j + 1) * FF_CHUNK))
            ccols = slice(2 * j * FF_CHUNK, 2 * (j + 1) * FF_CHUNK)
            buf = u_ref.at[(rb * n_chunks + j) % 2]
            buf[0:SUBLANES, :] = carry_ref[:, ccols]
            for half in range(2):
                buf[SUBLANES:SUBLANES + MM_ROWS, half * FF_CHUNK:(half + 1) * FF_CHUNK] = _dot(
                    h_ref[r, :], wup_ref[:, cols[half]])
            carry_ref[:, ccols] = buf[MM_ROWS:MM_ROWS + SUBLANES, :]
            for c in range(MM_ROWS // CHUNK):
                window = buf[c * CHUNK:c * CHUNK + SUBLANES + CHUNK, :]
                taps = [pltpu.roll(window, FFN_CONV_WIDTH - 1 - k, 0)[SUBLANES:] for k in range(FFN_CONV_WIDTH - 1)]
                taps.append(window[SUBLANES:])
                conv = []
                for half in range(2):
                    us = slice(half * FF_CHUNK, (half + 1) * FF_CHUNK)
                    acc = jnp.broadcast_to(cb_ref[:, cols[half]], (CHUNK, FF_CHUNK))
                    for k in range(FFN_CONV_WIDTH):
                        acc = acc + cw_ref[k:k + 1, cols[half]] * taps[k][:, us]
                    conv.append(acc)
                ar = slice(rb * MM_ROWS + c * CHUNK, rb * MM_ROWS + (c + 1) * CHUNK)
                act_ref[ar, cols[0]] = (jax.nn.silu(conv[0]) * conv[1]).astype(BF16)

        for cb in range(D_MODEL // MXU_DIM):
            cs = slice(cb * MXU_DIM, (cb + 1) * MXU_DIM)
            o_ref[0, r, cs] = x_ref[0, r, cs] + _dot(act_ref[r, :], wdn_ref[:, cs])
        if final_norm:
            o_ref[0, r, :] = _rms(o_ref[0, r, :], fg_ref[...])


def _conv_ffn(x, norm_g, w_up, conv_w, conv_b, w_down, final_g, *, final_norm):
    bn, s, _ = x.shape
    rows = FFN_ROW_TILE
    row = lambda a: a.reshape(1, -1)
    tile = pl.BlockSpec((1, rows, D_MODEL), lambda b, i: (b, i, 0))
    return pl.pallas_call(
        functools.partial(_ffn_body, rows=rows, final_norm=final_norm),
        grid=(bn, s // rows),
        in_specs=[tile, _const_spec((1, D_MODEL)), _const_spec((D_MODEL, 2 * D_FF)),
                  _const_spec((FFN_CONV_WIDTH, 2 * D_FF)), _const_spec((1, 2 * D_FF)),
                  _const_spec((D_FF, D_MODEL)), _const_spec((1, D_MODEL))],
        out_specs=tile,
        out_shape=jax.ShapeDtypeStruct(x.shape, F32),
        scratch_shapes=[pltpu.VMEM((rows, D_MODEL), BF16),
                        pltpu.VMEM((2, SUBLANES + MM_ROWS, 2 * FF_CHUNK), F32),
                        pltpu.VMEM((SUBLANES, 2 * D_FF), F32),
                        pltpu.VMEM((rows, D_FF), BF16)],
        compiler_params=_params(("arbitrary", "arbitrary")),
        name="conv_ffn_final" if final_norm else "conv_ffn",
    )(x, row(norm_g), w_up.astype(BF16), conv_w, row(conv_b), w_down.astype(BF16), row(final_g))


def _qkv_body(x_ref, ng_ref, w_ref, cos_ref, sin_ref, *refs, rows):
    n_dil = len(DILS)
    q_refs = (None,) * (n_dil - 1) + (refs[0],)
    k_refs = refs[1:1 + n_dil]
    v_refs = refs[1 + n_dil:1 + 2 * n_dil]
    h_ref, zs_ref = refs[1 + 2 * n_dil:]
    for c in range(rows // CHUNK):
        r = slice(c * CHUNK, (c + 1) * CHUNK)
        h_ref[r, :] = _rms(x_ref[0, r, :], ng_ref[...]).astype(BF16)

    half = HEAD_DIM // 2
    first_half = (lax.broadcasted_iota(jnp.int32, (MM_ROWS, MXU_DIM), 1) & (HEAD_DIM - 1)) < half
    q_scale = (HEAD_DIM ** -0.5) * math.log2(math.e)
    for which, out_refs in enumerate((q_refs, k_refs, v_refs)):
        for rb in range(rows // MM_ROWS):
            r = slice(rb * MM_ROWS, (rb + 1) * MM_ROWS)
            cos = jnp.concatenate([cos_ref[r, :]] * (MXU_DIM // LANES), axis=1)
            sin = jnp.concatenate([sin_ref[r, :]] * (MXU_DIM // LANES), axis=1)
            for cb in range(D_MODEL // MXU_DIM):
                ws = slice(which * D_MODEL + cb * MXU_DIM, which * D_MODEL + (cb + 1) * MXU_DIM)
                z = _dot(h_ref[r, :], w_ref[:, ws])
                if which < 2:
                    partner = jnp.where(first_half, pltpu.roll(z, MXU_DIM - half, 1), pltpu.roll(z, half, 1))
                    z = z * cos + partner * sin
                if which == 0:
                    z = z * q_scale
                for sub in range(MXU_DIM // LANES):
                    pair = cb * (MXU_DIM // LANES) + sub
                    zs_ref[which, 0, pair, r, :] = z[:, sub * LANES:(sub + 1) * LANES]
        for pair in range(N_PAIRS):
            for lvl, dil in enumerate(DILS):
                n = rows // dil
                if lvl > 0:
                    prev_dil = DILS[lvl - 1]
                    ratio, prev_n = dil // prev_dil, rows // prev_dil
                    for prev_res in range(prev_dil):
                        for a in range(ratio):
                            res = prev_res + prev_dil * a
                            zs_ref[which, lvl, pair, res * n:(res + 1) * n, :] = zs_ref[
                                which, lvl - 1, pair, pl.ds(prev_res * prev_n + a, n, stride=ratio), :]
                if out_refs[lvl] is not None:
                    for res in range(dil):
                        out_refs[lvl][0, pair, :, res * LANES:(res + 1) * LANES] = zs_ref[
                            which, lvl, pair, res * n:(res + 1) * n, :].astype(BF16)


def _qkv_rotary(x, norm_g, w_qkv):
    bn, s, _ = x.shape
    rows = ROW_TILE
    half = HEAD_DIM // 2
    inv = ROPE_THETA ** (-jnp.arange(half, dtype=F32) / half)
    ang = jnp.arange(s).astype(F32)[:, None] * inv[None, :]
    cos = jnp.tile(jnp.cos(ang), (1, LANES // half))
    sin = jnp.tile(jnp.concatenate([-jnp.sin(ang), jnp.sin(ang)], axis=1), (1, LANES // HEAD_DIM))
    tile = pl.BlockSpec((1, rows, D_MODEL), lambda b, i: (b, i, 0))
    tab = pl.BlockSpec((rows, LANES), lambda b, i: (i, 0))

    def view(dil):
        spec = pl.BlockSpec((1, N_PAIRS, rows // dil, dil * LANES), lambda b, i: (b, 0, i, 0))
        return spec, jax.ShapeDtypeStruct((bn, N_PAIRS, s // dil, dil * LANES), BF16)

    views = [view(DILS[-1])] + [view(d) for d in DILS] * 2
    outs = pl.pallas_call(
        functools.partial(_qkv_body, rows=rows),
        grid=(bn, s // rows),
        in_specs=[tile, _const_spec((1, D_MODEL)), _const_spec((D_MODEL, 3 * D_MODEL)), tab, tab],
        out_specs=[v[0] for v in views],
        out_shape=[v[1] for v in views],
        scratch_shapes=[pltpu.VMEM((rows, D_MODEL), BF16),
                        pltpu.VMEM((3, len(DILS), N_PAIRS, rows, LANES), F32)],
        compiler_params=_params(("arbitrary", "arbitrary")),
        name="qkv_rotary",
    )(x, norm_g.reshape(1, -1), w_qkv.astype(BF16), cos, sin)
    n_dil = len(DILS)
    return outs[0], outs[1:1 + n_dil], outs[1 + n_dil:1 + 2 * n_dil]


def _query_chunks(dil, res, n):
    per = MAX_DIL // dil
    rows = BLOCK // per
    return [(res + dil * a, rows * n, rows) for a in range(per)]


def _gather(ref, chunks, heads):
    parts = [ref[r, h * BLOCK + row0:h * BLOCK + row0 + n, :] for h in range(heads) for r, row0, n in chunks]
    return parts[0] if len(parts) == 1 else jnp.concatenate(parts, axis=0)


def _scatter(ref, chunks, heads, val):
    off = 0
    for h in range(heads):
        for r, row0, n in chunks:
            ref[r, h * BLOCK + row0:h * BLOCK + row0 + n, :] = val[off:off + n]
            off += n


def _attn_body(*refs):
    n_branch = len(DILATED_PATTERNS)
    q_ref = refs[0]
    kv_refs = refs[1:1 + 4 * n_branch]
    o_ref = refs[1 + 4 * n_branch]
    qs_ref, acc_ref, m_ref, l_ref, mask_ref, ost_ref = refs[2 + 4 * n_branch:]
    first_tile = pl.program_id(1) == 0

    left = lax.broadcasted_iota(jnp.int32, (BLOCK, LANES), 1) < HEAD_DIM
    for r in range(MAX_DIL):
        q = q_ref[0, 0, :, r * LANES:(r + 1) * LANES].astype(F32)
        qs_ref[r, 0:BLOCK, :] = jnp.where(left, q, 0.0)
        qs_ref[r, BLOCK:2 * BLOCK, :] = jnp.where(left, 0.0, q)

    row = lax.broadcasted_iota(jnp.int32, (BLOCK, 2 * BLOCK), 0)
    kj = lax.broadcasted_iota(jnp.int32, (BLOCK, 2 * BLOCK), 1)
    for bi, (window, dil) in enumerate(DILATED_PATTERNS):
        assert window // dil == BLOCK
        per = MAX_DIL // dil
        rows = BLOCK // per
        jq = (row >> (rows.bit_length() - 1)) + per * (row & (rows - 1))
        band = (kj >= jq) & (kj <= jq + BLOCK)
        mask_ref[bi, 0] = band.astype(F32)
        mask_ref[bi, 1] = (band & (kj >= jnp.where(first_tile, BLOCK, 0))).astype(F32)

    for bi, (window, dil) in enumerate(DILATED_PATTERNS):
        kc_ref, kp_ref, vc_ref, vp_ref = kv_refs[4 * bi:4 * bi + 4]
        first, last = bi == 0, bi == n_branch - 1
        assert not last or dil == MAX_DIL
        for res in range(dil):
            ls = slice(res * LANES, (res + 1) * LANES)
            for n in range(MAX_DIL // dil):
                chunks = _query_chunks(dil, res, n)
                if n == 0:
                    k = jnp.concatenate([kp_ref[0, 0, :, ls], kc_ref[0, 0, 0:BLOCK, ls]], axis=0)
                    v = jnp.concatenate([vp_ref[0, 0, :, ls], vc_ref[0, 0, 0:BLOCK, ls]], axis=0)
                else:
                    k = kc_ref[0, 0, (n - 1) * BLOCK:(n + 1) * BLOCK, ls]
                    v = vc_ref[0, 0, (n - 1) * BLOCK:(n + 1) * BLOCK, ls]
                valid = mask_ref[bi, 1 if n == 0 else 0] > 0.5
                valid = jnp.concatenate([valid, valid], axis=0)

                q = _gather(qs_ref, chunks, 2).astype(BF16)
                s = jnp.where(valid, _dot_nt(q, k), NEG)
                sa, sb = s[:, :LANES], s[:, LANES:]
                mx = jnp.broadcast_to(jnp.max(jnp.maximum(sa, sb), axis=-1, keepdims=True), sa.shape)
                if first:
                    m_new = mx
                else:
                    m_old = _gather(m_ref, chunks, 2)
                    m_new = jnp.maximum(m_old, mx)
                pa = jnp.exp2(sa - m_new)
                pb = jnp.exp2(sb - m_new)
                l_new = jnp.broadcast_to(jnp.sum(pa + pb, axis=-1, keepdims=True), sa.shape)
                pv = _dot(jnp.concatenate([pa, pb], axis=1).astype(BF16), v)
                acc = jnp.where(left, pv[:BLOCK], pv[BLOCK:])
                if not first:
                    alpha = jnp.exp2(m_old - m_new)
                    l_new = alpha * _gather(l_ref, chunks, 2) + l_new
                    acc = acc + _gather(acc_ref, chunks, 1) * jnp.where(left, alpha[:BLOCK], alpha[BLOCK:])
                if last:
                    out = acc / jnp.where(left, l_new[:BLOCK], l_new[BLOCK:])
                    prev_dil = DILS[-2]
                    a, prev_res = divmod(res, prev_dil)
                    ost_ref[len(DILS) - 2, pl.ds(prev_res * (ATTN_TILE // prev_dil) + a, BLOCK,
                                                 stride=dil // prev_dil), :] = out
                else:
                    _scatter(m_ref, chunks, 2, m_new)
                    _scatter(l_ref, chunks, 2, l_new)
                    _scatter(acc_ref, chunks, 1, acc)

    for lvl in range(len(DILS) - 2, 0, -1):
        dil, prev_dil = DILS[lvl], DILS[lvl - 1]
        n, prev_n, ratio = ATTN_TILE // dil, ATTN_TILE // prev_dil, dil // prev_dil
        for res in range(dil):
            a, prev_res = divmod(res, prev_dil)
            ost_ref[lvl - 1, pl.ds(prev_res * prev_n + a, n, stride=ratio), :] = ost_ref[lvl, res * n:(res + 1) * n, :]
    o_ref[0, 0] = ost_ref[0].astype(BF16)


def _dilated_attention(q, ks, vs):
    bn, n_pairs = q.shape[:2]
    s = ks[0].shape[2]
    tile = ATTN_TILE

    def cur_spec(dil):
        return pl.BlockSpec((1, 1, tile // dil, dil * LANES), lambda b, i, p: (b, p, i, 0))

    def prev_spec(dil):
        per = tile // dil // BLOCK
        return pl.BlockSpec((1, 1, BLOCK, dil * LANES), lambda b, i, p: (b, p, jnp.maximum(per * i - 1, 0), 0))

    operands, in_specs = [q], [cur_spec(MAX_DIL)]
    for bi, (_, dil) in enumerate(DILATED_PATTERNS):
        assert dil == DILS[bi]
        for a in (ks[bi], vs[bi]):
            operands += [a, a]
            in_specs += [cur_spec(dil), prev_spec(dil)]
    return pl.pallas_call(
        _attn_body,
        grid=(bn, s // tile, n_pairs),
        in_specs=in_specs,
        out_specs=cur_spec(1),
        out_shape=jax.ShapeDtypeStruct((bn, n_pairs, s, LANES), BF16),
        scratch_shapes=[pltpu.VMEM((MAX_DIL, 2 * BLOCK, LANES), F32), pltpu.VMEM((MAX_DIL, BLOCK, LANES), F32),
                        pltpu.VMEM((MAX_DIL, 2 * BLOCK, LANES), F32), pltpu.VMEM((MAX_DIL, 2 * BLOCK, LANES), F32),
                        pltpu.VMEM((len(DILATED_PATTERNS), 2, BLOCK, 2 * BLOCK), F32),
                        pltpu.VMEM((len(DILS) - 1, tile, LANES), F32)],
        compiler_params=_params(("arbitrary", "arbitrary", "arbitrary")),
        name="dilated_attention",
    )(*operands)


def _oproj_body(x_ref, a_ref, w_ref, o_ref, *, rows):
    for rb in range(rows // MM_ROWS):
        r = slice(rb * MM_ROWS, (rb + 1) * MM_ROWS)
        a = jnp.concatenate([a_ref[0, p, r, :] for p in range(N_PAIRS)], axis=1)
        for cb in range(D_MODEL // MXU_DIM):
            cs = slice(cb * MXU_DIM, (cb + 1) * MXU_DIM)
            o_ref[0, r, cs] = x_ref[0, r, cs] + _dot(a, w_ref[:, cs])


def _attn_out_proj(x, a, w_o):
    bn, s, _ = x.shape
    rows = ROW_TILE
    tile = pl.BlockSpec((1, rows, D_MODEL), lambda b, i: (b, i, 0))
    a_spec = pl.BlockSpec((1, N_PAIRS, rows, LANES), lambda b, i: (b, 0, i, 0))
    return pl.pallas_call(
        functools.partial(_oproj_body, rows=rows),
        grid=(bn, s // rows),
        in_specs=[tile, a_spec, _const_spec((D_MODEL, D_MODEL))],
        out_specs=tile,
        out_shape=jax.ShapeDtypeStruct(x.shape, F32),
        compiler_params=_params(("arbitrary", "arbitrary")),
        name="attn_out_proj",
    )(x, a, w_o.astype(BF16))


def kernel(x, even_norm_g, even_w_in, even_b_in, even_v_ln_g, even_v_ln_b, even_w_s, even_b_s, even_conv_w,
           even_conv_b, even_conv_ln_g, even_conv_ln_b, even_w_out, odd_norm_g, odd_w_qkv, odd_w_o, ffn_norm_g,
           ffn_w_up, ffn_conv_w, ffn_conv_b, ffn_w_down, final_norm_g):
    depth = ffn_norm_g.shape[0]
    for i in range(depth):
        j = i // 2
        if i % 2 == 0:
            x = _even_mixer(x, even_norm_g[j], even_w_in[j], even_b_in[j], even_v_ln_g[j], even_v_ln_b[j],
                            even_w_s[j], even_b_s[j], even_conv_w[j], even_conv_b[j], even_conv_ln_g[j],
                            even_conv_ln_b[j], even_w_out[j])
        else:
            q, ks, vs = _qkv_rotary(x, odd_norm_g[j], odd_w_qkv[j])
            x = _attn_out_proj(x, _dilated_attention(q, ks, vs), odd_w_o[j])
        x = _conv_ffn(x, ffn_norm_g[i], ffn_w_up[i], ffn_conv_w[i], ffn_conv_b[i], ffn_w_down[i],
                      final_norm_g, final_norm=(i == depth - 1))
    return x
```
